```python
import jax, jax.numpy as jnp
from jax import lax
import numpy as np

D_MODEL = 4096
BATCH = 4
SEQ = 2048
DEPTH = 4
DEC_BATCH = 16
DEC_SEQ = 64
PAST_LEN = 2048

CHUNK = 64
D_LRU = D_MODEL // 2
N_LRU_HEADS = 16
LRU_HEAD_DIM = D_LRU // N_LRU_HEADS
LRU_CONV_W = 4
LRU_C = 8.0
D_POOL = D_MODEL - D_LRU
POOL_WINDOWS = (2, 4, 8, 16)
N_POOL_GROUPS = len(POOL_WINDOWS)
POOL_GROUP_DIM = D_POOL // N_POOL_GROUPS
POOL_BUF = max(POOL_WINDOWS) - 1
D_MIX = D_LRU + D_POOL
D_IN = 2 * D_LRU + D_POOL
D_FF = 3 * D_MODEL
FFN_CONV_W = 3
EPS = 1e-6

kernel_name = "hawk_pool_hybrid_stream_step"


def _rms_norm(x, g):
    xf = x.astype(jnp.float32)
    y = xf * lax.rsqrt(jnp.mean(xf * xf, axis=-1, keepdims=True) + EPS)
    return (y * g.astype(jnp.float32)).astype(x.dtype)


def _causal_dwconv(x, buf, w, b):
    k = w.shape[0]
    t = x.shape[1]
    xp = jnp.concatenate([buf, x], axis=1)
    y = b + sum(w[j] * xp[:, j:j + t] for j in range(k))
    return y.astype(x.dtype), xp[:, -(k - 1):]


def _rg_lru(c, h0, pos, w_a, b_a, w_x, b_x, lam):
    bsz, t, _ = c.shape
    ch = c.reshape(bsz, t, N_LRU_HEADS, LRU_HEAD_DIM)
    ga = jnp.einsum('bthi,hij->bthj', ch, w_a).reshape(bsz, t, D_LRU) + b_a
    gx = jnp.einsum('bthi,hij->bthj', ch, w_x).reshape(bsz, t, D_LRU) + b_x
    r = jax.nn.sigmoid(ga.astype(jnp.float32))
    i = jax.nn.sigmoid(gx.astype(jnp.float32))
    log_a = -LRU_C * r * jax.nn.softplus(-lam.astype(jnp.float32))
    a = jnp.exp(log_a)
    mult = jnp.sqrt(-jnp.expm1(2.0 * log_a))
    mult = jnp.where((pos == 0)[None, :, None], 1.0, mult)
    bt = mult * i * c.astype(jnp.float32)

    def step(h, ab):
        a_t, b_t = ab
        h = a_t * h + b_t
        return h, h

    h_last, hs = lax.scan(step, h0.astype(jnp.float32), (jnp.swapaxes(a, 0, 1), jnp.swapaxes(bt, 0, 1)))
    return jnp.swapaxes(hs, 0, 1).astype(c.dtype), h_last.astype(c.dtype)


def _pool_mixer(u, buf, pos, w_pool, scale):
    bsz, t, _ = u.shape
    up = jnp.concatenate([buf, u], axis=1)
    upf = up.astype(jnp.float32)
    cs = jnp.concatenate([jnp.zeros((bsz, 1, D_POOL), jnp.float32), jnp.cumsum(upf, axis=1)], axis=1)
    end = cs[:, POOL_BUF + 1:POOL_BUF + 1 + t]
    uf = u.astype(jnp.float32)
    outs = []
    for g, w in enumerate(POOL_WINDOWS):
        sl = slice(g * POOL_GROUP_DIM, (g + 1) * POOL_GROUP_DIM)
        start = cs[:, POOL_BUF + 1 - w:POOL_BUF + 1 - w + t, sl]
        cnt = jnp.minimum(pos + 1, w).astype(jnp.float32)[None, :, None]
        outs.append((end[..., sl] - start) / cnt - uf[..., sl])
    d = jnp.stack(outs, axis=2).astype(u.dtype)
    y = jnp.einsum('btgi,gij->btgj', d, w_pool).reshape(bsz, t, D_POOL) * scale
    return y, up[:, -POOL_BUF:]


def _layer(x, pos, lru_conv_buf, h0, pool_buf, ffn_buf, p):
    (g_pre_mix, w_in, w_lru_conv, b_lru_conv, w_lru_a, b_lru_a, w_lru_x, b_lru_x, lru_lambda,
     w_pool, pool_scale, g_out_lru, g_out_pool, w_out, g_post_mix, g_pre_ffn, w_gate, w_up,
     w_ffn_conv, b_ffn_conv, w_down, g_post_ffn) = p
    xn = _rms_norm(x, g_pre_mix)
    z = xn @ w_in
    u_x = z[..., :D_LRU]
    u_g = z[..., D_LRU:2 * D_LRU]
    u_p = z[..., 2 * D_LRU:]
    c, new_lru_conv = _causal_dwconv(u_x, lru_conv_buf, w_lru_conv, b_lru_conv)
    y_lru, h_last = _rg_lru(c, h0, pos, w_lru_a, b_lru_a, w_lru_x, b_lru_x, lru_lambda)
    y_lru = y_lru * jax.nn.gelu(u_g)
    y_pool, new_pool = _pool_mixer(u_p, pool_buf, pos, w_pool, pool_scale)
    m = jnp.concatenate([_rms_norm(y_lru, g_out_lru), _rms_norm(y_pool, g_out_pool)], axis=-1) @ w_out
    x = x + _rms_norm(m, g_post_mix)
    xn = _rms_norm(x, g_pre_ffn)
    hg = xn @ w_gate
    hu = xn @ w_up
    hc, new_ffn = _causal_dwconv(hg, ffn_buf, w_ffn_conv, b_ffn_conv)
    f = (jax.nn.gelu(hc) * hu) @ w_down
    x = x + _rms_norm(f, g_post_ffn)
    return x, h_last, new_lru_conv, new_pool, new_ffn


def setup_inputs(seed: int = 0) -> dict:
    key = jax.random.key(seed)
    ks = jax.random.split(key, 32)
    f32 = jnp.float32
    n = lambda k, shp, s: jax.random.normal(k, shp, f32) * s
    gain = lambda k, shp: 1.0 + 0.05 * jax.random.normal(k, shp, f32)
    u = jax.random.uniform(ks[12], (DEPTH, D_LRU), f32, 0.9, 0.999)
    a0 = u ** (1.0 / LRU_C)
    lam = jnp.log(a0) - jnp.log1p(-a0)
    return {
        "x_prompt": n(ks[0], (BATCH, SEQ, D_MODEL), 1.0),
        "x_sample": n(ks[1], (DEC_BATCH, DEC_SEQ, D_MODEL), 1.0),
        "state_lru_h": n(ks[2], (DEPTH, DEC_BATCH, D_LRU), 0.5),
        "state_lru_conv": n(ks[3], (DEPTH, DEC_BATCH, LRU_CONV_W - 1, D_LRU), 1.0),
        "state_pool": n(ks[4], (DEPTH, DEC_BATCH, POOL_BUF, D_POOL), 1.0),
        "state_ffn_conv": n(ks[5], (DEPTH, DEC_BATCH, FFN_CONV_W - 1, D_FF), 1.0),
        "g_pre_mix": gain(ks[6], (DEPTH, D_MODEL)),
        "w_in": n(ks[7], (DEPTH, D_MODEL, D_IN), D_MODEL ** -0.5),
        "w_lru_conv": n(ks[8], (DEPTH, LRU_CONV_W, D_LRU), LRU_CONV_W ** -0.5),
        "b_lru_conv": n(ks[9], (DEPTH, D_LRU), 0.01),
        "w_lru_a": n(ks[10], (DEPTH, N_LRU_HEADS, LRU_HEAD_DIM, LRU_HEAD_DIM), LRU_HEAD_DIM ** -0.5),
        "b_lru_a": n(ks[11], (DEPTH, D_LRU), 0.01),
        "w_lru_x": n(ks[13], (DEPTH, N_LRU_HEADS, LRU_HEAD_DIM, LRU_HEAD_DIM), LRU_HEAD_DIM ** -0.5),
        "b_lru_x": n(ks[14], (DEPTH, D_LRU), 0.01),
        "lru_lambda": lam,
        "w_pool": n(ks[15], (DEPTH, N_POOL_GROUPS, POOL_GROUP_DIM, POOL_GROUP_DIM), POOL_GROUP_DIM ** -0.5),
        "pool_scale": gain(ks[16], (DEPTH, D_POOL)),
        "g_out_lru": gain(ks[17], (DEPTH, D_LRU)),
        "g_out_pool": gain(ks[18], (DEPTH, D_POOL)),
        "w_out": n(ks[19], (DEPTH, D_MIX, D_MODEL), D_MIX ** -0.5),
        "g_post_mix": gain(ks[20], (DEPTH, D_MODEL)),
        "g_pre_ffn": gain(ks[21], (DEPTH, D_MODEL)),
        "w_gate": n(ks[22], (DEPTH, D_MODEL, D_FF), D_MODEL ** -0.5),
        "w_up": n(ks[23], (DEPTH, D_MODEL, D_FF), D_MODEL ** -0.5),
        "w_ffn_conv": n(ks[24], (DEPTH, FFN_CONV_W, D_FF), FFN_CONV_W ** -0.5),
        "b_ffn_conv": n(ks[25], (DEPTH, D_FF), 0.01),
        "w_down": n(ks[26], (DEPTH, D_FF, D_MODEL), D_FF ** -0.5),
        "g_post_ffn": gain(ks[27], (DEPTH, D_MODEL)),
    }


def reference(x_prompt, x_sample, state_lru_h, state_lru_conv, state_pool, state_ffn_conv,
              g_pre_mix, w_in, w_lru_conv, b_lru_conv, w_lru_a, b_lru_a, w_lru_x, b_lru_x,
              lru_lambda, w_pool, pool_scale, g_out_lru, g_out_pool, w_out, g_post_mix,
              g_pre_ffn, w_gate, w_up, w_ffn_conv, b_ffn_conv, w_down, g_post_ffn):
    bp, tp, _ = x_prompt.shape
    ts = x_sample.shape[1]
    dt = x_prompt.dtype
    pos_p = jnp.arange(tp)
    pos_s = PAST_LEN + jnp.arange(ts)
    yp = x_prompt
    ys = x_sample
    p_h, p_c, p_p, p_f = [], [], [], []
    s_h, s_c, s_p, s_f = [], [], [], []
    for l in range(DEPTH):
        prm = (g_pre_mix[l], w_in[l], w_lru_conv[l], b_lru_conv[l], w_lru_a[l], b_lru_a[l],
               w_lru_x[l], b_lru_x[l], lru_lambda[l], w_pool[l], pool_scale[l], g_out_lru[l],
               g_out_pool[l], w_out[l], g_post_mix[l], g_pre_ffn[l], w_gate[l], w_up[l],
               w_ffn_conv[l], b_ffn_conv[l], w_down[l], g_post_ffn[l])
        yp, h, c, pb, fb = _layer(
            yp, pos_p,
            jnp.zeros((bp, LRU_CONV_W - 1, D_LRU), dt), jnp.zeros((bp, D_LRU), dt),
            jnp.zeros((bp, POOL_BUF, D_POOL), dt), jnp.zeros((bp, FFN_CONV_W - 1, D_FF), dt), prm)
        p_h.append(h); p_c.append(c); p_p.append(pb); p_f.append(fb)
        ys, h, c, pb, fb = _layer(ys, pos_s, state_lru_conv[l], state_lru_h[l], state_pool[l],
                                  state_ffn_conv[l], prm)
        s_h.append(h); s_c.append(c); s_p.append(pb); s_f.append(fb)
    return (yp, ys,
            jnp.stack(p_h), jnp.stack(p_c), jnp.stack(p_p), jnp.stack(p_f),
            jnp.stack(s_h), jnp.stack(s_c), jnp.stack(s_p), jnp.stack(s_f))
```

```python
import functools
import math

import jax
import jax.numpy as jnp
from jax import lax
from jax.experimental import pallas as pl
from jax.experimental.pallas import tpu as pltpu

PAST_LEN = 2048
LRU_C = 8.0
POOL_WINDOWS = (2, 4, 8, 16)
EPS = 1e-6

_F32 = jnp.float32
_BF16 = jnp.bfloat16
_VMEM_LIMIT = 56 * 1024 * 1024


def _pick(n, candidates):
    for c in candidates:
        if n % c == 0:
            return c
    raise ValueError(f"no tile in {candidates} divides {n}")


def _params(*sem):
    return pltpu.CompilerParams(dimension_semantics=sem, vmem_limit_bytes=_VMEM_LIMIT)


def _rms(x, g):
    return x * lax.rsqrt(jnp.mean(x * x, axis=-1, keepdims=True) + EPS) * g


def _norm_kernel(x_ref, g_ref, o_ref):
    o_ref[...] = _rms(x_ref[...], g_ref[...]).astype(o_ref.dtype)


def _norm_cast(x, g):
    r, d = x.shape
    tr = _pick(r, (512, 256, 128, 64))
    return pl.pallas_call(
        _norm_kernel,
        out_shape=jax.ShapeDtypeStruct((r, d), _BF16),
        grid=(r // tr,),
        in_specs=[pl.BlockSpec((tr, d), lambda i: (i, 0)),
                  pl.BlockSpec((1, d), lambda i: (0, 0))],
        out_specs=pl.BlockSpec((tr, d), lambda i: (i, 0)),
        compiler_params=_params("parallel"),
        name="norm_cast",
    )(x, g)


def _resid_kernel(x_ref, m_ref, gp_ref, gn_ref, xo_ref, xn_ref):
    x = x_ref[...] + _rms(m_ref[...], gp_ref[...])
    xo_ref[...] = x
    xn_ref[...] = _rms(x, gn_ref[...]).astype(xn_ref.dtype)


def _resid_last_kernel(x_ref, m_ref, gp_ref, xo_ref):
    xo_ref[...] = x_ref[...] + _rms(m_ref[...], gp_ref[...])


def _resid_norm(x, m, g_post, g_next):
    r, d = x.shape
    tr = _pick(r, (256, 128, 64))
    row = pl.BlockSpec((tr, d), lambda i: (i, 0))
    vec = pl.BlockSpec((1, d), lambda i: (0, 0))
    if g_next is None:
        return pl.pallas_call(
            _resid_last_kernel,
            out_shape=jax.ShapeDtypeStruct((r, d), _F32),
            grid=(r // tr,), in_specs=[row, row, vec], out_specs=row,
            compiler_params=_params("parallel"), name="resid_last",
        )(x, m, g_post), None
    return pl.pallas_call(
        _resid_kernel,
        out_shape=(jax.ShapeDtypeStruct((r, d), _F32), jax.ShapeDtypeStruct((r, d), _BF16)),
        grid=(r // tr,), in_specs=[row, row, vec, vec], out_specs=(row, row),
        compiler_params=_params("parallel"), name="resid_norm",
    )(x, m, g_post, g_next)


def _mm_kernel(x_ref, w_ref, o_ref):
    o_ref[...] = jnp.dot(x_ref[...], w_ref[...], preferred_element_type=_F32)


def _matmul_fullk(x, w_stack, layer, tm):
    r, k = x.shape
    n = w_stack.shape[2]
    tn = _pick(n, (1024, 512, 256, 128))
    return pl.pallas_call(
        _mm_kernel,
        out_shape=jax.ShapeDtypeStruct((r, n), _F32),
        grid=(r // tm, n // tn),
        in_specs=[pl.BlockSpec((tm, k), lambda i, j: (i, 0)),
                  pl.BlockSpec((None, k, tn), lambda i, j: (layer, 0, j))],
        out_specs=pl.BlockSpec((tm, tn), lambda i, j: (i, j)),
        compiler_params=_params("parallel", "parallel"),
        name="matmul_fullk",
    )(x, w_stack)


def _mm_acc_kernel(x_ref, w_ref, o_ref):
    @pl.when(pl.program_id(2) == 0)
    def _():
        o_ref[...] = jnp.zeros_like(o_ref)

    o_ref[...] += jnp.dot(x_ref[...], w_ref[...], preferred_element_type=_F32)


def _matmul_ktiled(x, w_stack, layer, tm):
    r, k = x.shape
    n = w_stack.shape[2]
    tn = _pick(n, (1024, 512, 256, 128))
    tk = _pick(k, (2048, 1024, 512, 256, 128))
    return pl.pallas_call(
        _mm_acc_kernel,
        out_shape=jax.ShapeDtypeStruct((r, n), _F32),
        grid=(r // tm, n // tn, k // tk),
        in_specs=[pl.BlockSpec((tm, tk), lambda i, j, kk: (i, kk)),
                  pl.BlockSpec((None, tk, tn), lambda i, j, kk: (layer, kk, j))],
        out_specs=pl.BlockSpec((tm, tn), lambda i, j, kk: (i, j)),
        compiler_params=_params("parallel", "parallel", "arbitrary"),
        name="matmul_ktiled",
    )(x, w_stack)


def _softplus(x):
    return jnp.maximum(x, 0.0) + jnp.log1p(jnp.exp(-jnp.abs(x)))


def _mixer_kernel(ux_ref, ug_ref, up_ref, cst_ref, hst_ref, pst_ref,
                  wconv_ref, bconv_ref, wax_ref, ba_ref, bx_ref, lam_ref,
                  wpool_ref, pscale_ref, glru_ref, gpool_ref, mix_in_ref,
                  mix_ref, nconv_ref, nh_ref, npool_ref,
                  cx, px, a_s, b_s, yp_s, h_c,
                  *, t, pos0, heads, hd, windows, gd, rc):
    del mix_in_ref
    j = pl.program_id(1)
    dl = heads * hd
    kc = cx.shape[0] - t
    kp = px.shape[0] - t
    nb = kp - 1

    @pl.when(j == 0)
    def _():
        cx[kc - 3:kc, :] = cst_ref[0]
        px[1:kp, :] = pst_ref[0]
        h_c[...] = hst_ref[0]

    @pl.when(j > 0)
    def _():
        cx[kc - 3:kc, :] = cx[t + kc - 3:t + kc, :]
        px[1:kp, :] = px[t + 1:t + kp, :]

    cx[kc:kc + t, :] = ux_ref[...]
    px[kp:kp + t, :] = up_ref[...]

    pos = pos0 + j * t + lax.broadcasted_iota(jnp.int32, (t, 1), 0)
    first = pos == 0
    sp_all = _softplus(-lam_ref[...])

    for h in range(heads):
        sl = slice(h * hd, (h + 1) * hd)
        c = bconv_ref[:, sl] + (wconv_ref[0:1, sl] * cx[kc - 3:kc - 3 + t, sl]
                                + wconv_ref[1:2, sl] * cx[kc - 2:kc - 2 + t, sl]
                                + wconv_ref[2:3, sl] * cx[kc - 1:kc - 1 + t, sl]
                                + wconv_ref[3:4, sl] * cx[kc:kc + t, sl])
        g = jnp.dot(c.astype(_BF16), wax_ref[h], preferred_element_type=_F32)
        r = jax.nn.sigmoid(g[:, :hd] + ba_ref[:, sl])
        i = jax.nn.sigmoid(g[:, hd:] + bx_ref[:, sl])
        log_a = (-LRU_C) * r * sp_all[:, sl]
        a = jnp.exp(log_a)
        mult = jnp.sqrt(-jnp.tanh(log_a) * (a * a + 1.0))
        mult = jnp.where(first, 1.0, mult)
        a_s[:, sl] = a
        b_s[:, sl] = mult * i * c

    def step(s, h):
        h = a_s[pl.ds(s, 1), :] * h + b_s[pl.ds(s, 1), :]
        b_s[pl.ds(s, 1), :] = h
        return h

    h_last = lax.fori_loop(0, t, step, h_c[...], unroll=8)
    h_c[...] = h_last

    cntf = [jnp.minimum(pos + 1, w).astype(_F32) for w in windows]
    for gi, w in enumerate(windows):
        sl = slice(gi * gd, (gi + 1) * gd)
        s0 = kp + 1 - w
        acc = px[s0:s0 + t, sl]
        for k in range(1, w):
            acc = acc + px[s0 + k:s0 + k + t, sl]
        d = acc / cntf[gi] - px[kp:kp + t, sl]
        yp = jnp.dot(d.astype(_BF16), wpool_ref[gi], preferred_element_type=_F32)
        yp_s[:, sl] = yp * pscale_ref[:, sl]

    def rows(q, carry):
        r0 = pl.multiple_of(q * rc, rc)
        y = b_s[pl.ds(r0, rc), :] * jax.nn.gelu(ug_ref[pl.ds(r0, rc), :])
        mix_ref[pl.ds(r0, rc), :dl] = _rms(y, glru_ref[...]).astype(mix_ref.dtype)
        mix_ref[pl.ds(r0, rc), dl:] = _rms(yp_s[pl.ds(r0, rc), :], gpool_ref[...]).astype(mix_ref.dtype)
        return carry

    lax.fori_loop(0, t // rc, rows, 0)

    nconv_ref[0] = cx[t + kc - 3:t + kc, :]
    npool_ref[0] = px[t + 1:t + kp, :]
    nh_ref[0] = h_last


def _mixer(z, mix_in, row0, nb, tl, t, pos0, cst, hst, pst, wconv, bconv, wax, ba, bx, lam,
           wpool, pscale, glru, gpool):
    heads, hd = wax.shape[0], wax.shape[1]
    g, gd = wpool.shape[0], wpool.shape[1]
    dl, dp = heads * hd, g * gd
    assert dl == dp, "column blocks of z assume equal head-group widths"
    nt = tl // t
    rb0 = row0 // t
    r = z.shape[0]
    nbuf = max(POOL_WINDOWS) - 1

    def zspec(col):
        return pl.BlockSpec((t, dl), lambda b, j: (rb0 + b * nt + j, col))

    def vec(n):
        return pl.BlockSpec((1, n), lambda b, j: (0, 0))

    def full(a):
        return pl.BlockSpec(a.shape, lambda b, j: (0,) * a.ndim)

    def st(rows, n):
        return pl.BlockSpec((1, rows, n), lambda b, j: (b, 0, 0))

    kern = functools.partial(_mixer_kernel, t=t, pos0=pos0, heads=heads, hd=hd,
                             windows=POOL_WINDOWS, gd=gd, rc=16)
    return pl.pallas_call(
        kern,
        out_shape=(jax.ShapeDtypeStruct((r, dl + dp), _BF16),
                   jax.ShapeDtypeStruct((nb, 3, dl), _F32),
                   jax.ShapeDtypeStruct((nb, 1, dl), _F32),
                   jax.ShapeDtypeStruct((nb, nbuf, dp), _F32)),
        grid=(nb, nt),
        in_specs=[zspec(0), zspec(1), zspec(2), st(3, dl), st(1, dl), st(nbuf, dp),
                  full(wconv), vec(dl), full(wax), vec(dl), vec(dl), vec(dl),
                  full(wpool), vec(dp), vec(dl), vec(dp),
                  pl.BlockSpec(memory_space=pl.ANY)],
        out_specs=(pl.BlockSpec((t, dl + dp), lambda b, j: (rb0 + b * nt + j, 0)),
                   st(3, dl), st(1, dl), st(nbuf, dp)),
        scratch_shapes=[pltpu.VMEM((8 + t, dl), _F32), pltpu.VMEM((16 + t, dp), _F32),
                        pltpu.VMEM((t, dl), _F32), pltpu.VMEM((t, dl), _F32),
                        pltpu.VMEM((t, dp), _F32), pltpu.VMEM((1, dl), _F32)],
        input_output_aliases={16: 0},
        compiler_params=_params("parallel", "arbitrary"),
        name="mixer",
    )(z, z, z, cst, hst, pst, wconv, bconv, wax, ba, bx, lam, wpool, pscale, glru, gpool, mix_in)


def _ffn1_kernel(x_ref, wg_ref, wu_ref, wc_ref, bc_ref, sst_in_ref,
                 f_ref, pst_ref, sst_ref, hs, hu_s,
                 *, tm, n_pt, tiles_per_seq, ts, rc):
    i = pl.program_id(1)
    x = x_ref[...]
    hs[8:8 + tm, :] = jnp.dot(x, wg_ref[...], preferred_element_type=_F32)
    hu_s[...] = jnp.dot(x, wu_ref[...], preferred_element_type=_F32)
    w0, w1, w2 = wc_ref[0:1, :], wc_ref[1:2, :], wc_ref[2:3, :]
    bc = bc_ref[...]

    def conv_rows(r0, n):
        hc = bc + (w0 * hs[r0 + 6:r0 + 6 + n, :] + w1 * hs[r0 + 7:r0 + 7 + n, :]
                   + w2 * hs[r0 + 8:r0 + 8 + n, :])
        f_ref[r0:r0 + n, :] = (jax.nn.gelu(hc) * hu_s[r0:r0 + n, :]).astype(f_ref.dtype)

    @pl.when(i < n_pt)
    def _prompt():
        @pl.when(i % tiles_per_seq == 0)
        def _():
            hs[6:8, :] = jnp.zeros((2, hs.shape[1]), _F32)

        for r0 in range(0, tm, rc):
            conv_rows(r0, rc)
        tail = hs[tm + 6:tm + 8, :]
        pst_ref[0] = tail
        hs[6:8, :] = tail

    @pl.when(i >= n_pt)
    def _sample():
        segs = tm // ts
        for s in range(segs):
            sst_ref[s] = hs[8 + ts * (s + 1) - 2:8 + ts * (s + 1), :]
        for s in range(segs):
            hs[6 + ts * s:8 + ts * s, :] = sst_in_ref[s]
            conv_rows(ts * s, ts)


def _ffn1(xn, wg_stack, wu_stack, layer, wc, bc, sst_in, tm, n_pt, tiles_per_seq, nbp, ts):
    r, k = xn.shape
    n = wg_stack.shape[2]
    tn = _pick(n, (512, 256, 128))
    n_m = r // tm
    n_st = n_m - n_pt
    segs = tm // ts
    rc = _pick(tm, (128, 64))
    kern = functools.partial(_ffn1_kernel, tm=tm, n_pt=n_pt, tiles_per_seq=tiles_per_seq, ts=ts, rc=rc)
    last_p = nbp - 1
    return pl.pallas_call(
        kern,
        out_shape=(jax.ShapeDtypeStruct((r, n), _BF16),
                   jax.ShapeDtypeStruct((nbp, 2, n), _F32),
                   jax.ShapeDtypeStruct((n_st * segs, 2, n), _F32)),
        grid=(n // tn, n_m),
        in_specs=[pl.BlockSpec((tm, k), lambda j, i: (i, 0)),
                  pl.BlockSpec((None, k, tn), lambda j, i: (layer, 0, j)),
                  pl.BlockSpec((None, k, tn), lambda j, i: (layer, 0, j)),
                  pl.BlockSpec((3, tn), lambda j, i: (0, j)),
                  pl.BlockSpec((1, tn), lambda j, i: (0, j)),
                  pl.BlockSpec((segs, 2, tn), lambda j, i: (jnp.maximum(i - n_pt, 0), 0, j))],
        out_specs=(pl.BlockSpec((tm, tn), lambda j, i: (i, j)),
                   pl.BlockSpec((1, 2, tn), lambda j, i: (jnp.minimum(i // tiles_per_seq, last_p), 0, j)),
                   pl.BlockSpec((segs, 2, tn), lambda j, i: (jnp.maximum(i - n_pt, 0), 0, j))),
        scratch_shapes=[pltpu.VMEM((8 + tm, tn), _F32), pltpu.VMEM((tm, tn), _F32)],
        compiler_params=_params("parallel", "arbitrary"),
        name="ffn1",
    )(xn, wg_stack, wu_stack, wc, bc, sst_in)


def kernel(x_prompt, x_sample, state_lru_h, state_lru_conv, state_pool, state_ffn_conv, g_pre_mix, w_in, w_lru_conv, b_lru_conv, w_lru_a, b_lru_a, w_lru_x, b_lru_x, lru_lambda, w_pool, pool_scale, g_out_lru, g_out_pool, w_out, g_post_mix, g_pre_ffn, w_gate, w_up, w_ffn_conv, b_ffn_conv, w_down, g_post_ffn):
    bp, tp, d = x_prompt.shape
    bs, ts, _ = x_sample.shape
    depth = w_in.shape[0]
    rp, rs = bp * tp, bs * ts
    r = rp + rs
    dl = w_lru_conv.shape[2]
    dp = pool_scale.shape[1]
    dff = w_gate.shape[2]
    nbuf = max(POOL_WINDOWS) - 1

    tm = min(1024, math.gcd(rp, rs))
    assert tp % tm == 0 and tm % ts == 0 and rp % tm == 0 and rs % tm == 0
    t_mix = min(256, tp)

    x = jnp.concatenate([x_prompt.reshape(rp, d), x_sample.reshape(rs, d)], axis=0)

    w_in_b = w_in.astype(_BF16)
    w_out_b = w_out.astype(_BF16)
    w_gate_b = w_gate.astype(_BF16)
    w_up_b = w_up.astype(_BF16)
    w_down_b = w_down.astype(_BF16)
    wax_b = jnp.concatenate([w_lru_a, w_lru_x], axis=-1).astype(_BF16)
    w_pool_b = w_pool.astype(_BF16)

    zeros_c = jnp.zeros((bp, 3, dl), _F32)
    zeros_h = jnp.zeros((bp, 1, dl), _F32)
    zeros_p = jnp.zeros((bp, nbuf, dp), _F32)

    def v(a, l):
        return a[l].reshape(1, -1)

    p_h, p_c, p_p, p_f, s_h, s_c, s_p, s_f = ([] for _ in range(8))
    xn = _norm_cast(x, v(g_pre_mix, 0))
    for l in range(depth):
        z = _matmul_fullk(xn, w_in_b, l, tm)
        mix0 = jnp.zeros((r, dl + dp), _BF16) if l == 0 else mix
        common = (w_lru_conv[l], v(b_lru_conv, l), wax_b[l], v(b_lru_a, l), v(b_lru_x, l),
                  v(lru_lambda, l), w_pool_b[l], v(pool_scale, l), v(g_out_lru, l), v(g_out_pool, l))
        mix, c_p, h_p, pb_p = _mixer(z, mix0, 0, bp, tp, t_mix, 0, zeros_c, zeros_h, zeros_p, *common)
        mix, c_s, h_s, pb_s = _mixer(z, mix, rp, bs, ts, ts, PAST_LEN, state_lru_conv[l],
                                     state_lru_h[l].reshape(bs, 1, dl), state_pool[l], *common)
        m = _matmul_fullk(mix, w_out_b, l, tm)
        x, xn = _resid_norm(x, m, v(g_post_mix, l), v(g_pre_ffn, l))
        f, f_p, f_s = _ffn1(xn, w_gate_b, w_up_b, l, w_ffn_conv[l], v(b_ffn_conv, l),
                            state_ffn_conv[l], tm, rp // tm, tp // tm, bp, ts)
        y = _matmul_ktiled(f, w_down_b, l, tm)
        x, xn = _resid_norm(x, y, v(g_post_ffn, l), v(g_pre_mix, l + 1) if l + 1 < depth else None)
        p_h.append(h_p.reshape(bp, dl)); p_c.append(c_p); p_p.append(pb_p); p_f.append(f_p)
        s_h.append(h_s.reshape(bs, dl)); s_c.append(c_s); s_p.append(pb_s); s_f.append(f_s)

    return (x[:rp].reshape(bp, tp, d), x[rp:].reshape(bs, ts, d),
            jnp.stack(p_h), jnp.stack(p_c), jnp.stack(p_p), jnp.stack(p_f),
            jnp.stack(s_h), jnp.stack(s_c), jnp.stack(s_p), jnp.stack(s_f))
```
